```python
import jax, jax.numpy as jnp
from jax import lax
import numpy as np

D_MODEL = 2048
BATCH = 4
SEQ = 2048
DEPTH = 2

GRID_W = 64
CTX_LEN = 256

D_MLSTM = D_MODEL // 2
N_MLSTM_HEADS = 4
MLSTM_HEAD_DIM = D_MLSTM // N_MLSTM_HEADS
MLSTM_CHUNK = 64
QK_CONV = 3
N_DIR = 2
D_POOL = D_MODEL - D_MLSTM
POOL_WINDOWS = (2, 4, 8, 16)
N_POOL_GROUPS = len(POOL_WINDOWS)
POOL_GROUP_DIM = D_POOL // N_POOL_GROUPS
N_GATE = 2 * N_DIR * N_MLSTM_HEADS
N_IN_EVEN = 5 * D_MLSTM + N_GATE + 2 * D_POOL
EVEN_SPLITS = (D_MLSTM, 2 * D_MLSTM, 3 * D_MLSTM, 4 * D_MLSTM, 5 * D_MLSTM,
               5 * D_MLSTM + N_GATE, 5 * D_MLSTM + N_GATE + D_POOL)
D_SGU = D_MODEL
N_SGU_HEADS = 8
SGU_HEAD_DIM = D_SGU // N_SGU_HEADS
SGU_CHUNK = 128
N_IN_ODD = 3 * D_SGU

N_EVEN = (DEPTH + 1) // 2
N_ODD = DEPTH // 2
DEEPNORM_ALPHA = (2.0 * DEPTH) ** 0.25
DEEPNORM_BETA = (8.0 * DEPTH) ** -0.25
LN_EPS = 1e-5

kernel_name = "hybrid_mlstm_pool_sgu_deepnorm_prefix"


def _normalize(x):
    xf = x.astype(jnp.float32)
    mu = xf.mean(-1, keepdims=True)
    var = jnp.square(xf - mu).mean(-1, keepdims=True)
    return (xf - mu) * lax.rsqrt(var + LN_EPS)


def layer_norm(x, g, b):
    return (_normalize(x) * g + b).astype(x.dtype)


def ada_params(cond, w, b):
    mod = jax.nn.silu(cond) @ w + b
    return jnp.split(mod, 3, axis=-1)


def dwconv_centered(x, w):
    k, ch = w.shape
    return lax.conv_general_dilated(x, w[:, None, :].astype(x.dtype), window_strides=(1,),
                                    padding=[(k // 2, k // 2)],
                                    dimension_numbers=('NWC', 'WIO', 'NWC'),
                                    feature_group_count=ch)


def mlstm_scan(q, k, v, log_i, log_f, state):
    bsz, nh, seqlen, dh = q.shape
    nc = seqlen // MLSTM_CHUNK

    def to_chunks(a):
        a = a.astype(jnp.float32).reshape(bsz, nh, nc, MLSTM_CHUNK, *a.shape[3:])
        return jnp.moveaxis(a, 2, 0)

    lower = jnp.tril(jnp.ones((MLSTM_CHUNK, MLSTM_CHUNK), bool))

    def step(carry, inp):
        cmat, nvec, m = carry
        qc, kc, vc, ic, fc = inp
        b = jnp.cumsum(fc, axis=-1)
        d = b[..., :, None] - b[..., None, :] + ic[..., None, :]
        d = jnp.where(lower, d, -jnp.inf)
        inter = b + m[..., None]
        m_t = jnp.maximum(d.max(-1), inter)
        w_intra = jnp.exp(d - m_t[..., None])
        w_inter = jnp.exp(inter - m_t)
        s = jnp.einsum('bhtd,bhsd->bhts', qc, kc) * w_intra
        num = (jnp.einsum('bhts,bhsd->bhtd', s, vc)
               + w_inter[..., None] * jnp.einsum('bhvk,bhtk->bhtv', cmat, qc))
        den = s.sum(-1) + w_inter * jnp.einsum('bhk,bhtk->bht', nvec, qc)
        h = num / jnp.maximum(jnp.abs(den), jnp.exp(-m_t))[..., None]
        b_last = b[..., -1]
        g = b_last[..., None] - b + ic
        m_new = jnp.maximum(b_last + m, g.max(-1))
        w_s = jnp.exp(g - m_new[..., None])
        decay = jnp.exp(b_last + m - m_new)
        c_new = decay[..., None, None] * cmat + jnp.einsum('bhs,bhsv,bhsk->bhvk', w_s, vc, kc)
        n_new = decay[..., None] * nvec + jnp.einsum('bhs,bhsk->bhk', w_s, kc)
        return (c_new, n_new, m_new), h

    xs = tuple(to_chunks(a) for a in (q, k, v, log_i, log_f))
    state, h = lax.scan(step, state, xs)
    h = jnp.moveaxis(h, 0, 2).reshape(bsz, nh, seqlen, dh)
    return h, state


def zero_mlstm_state(bsz):
    return (jnp.zeros((bsz, N_MLSTM_HEADS, MLSTM_HEAD_DIM, MLSTM_HEAD_DIM), jnp.float32),
            jnp.zeros((bsz, N_MLSTM_HEADS, MLSTM_HEAD_DIM), jnp.float32),
            jnp.zeros((bsz, N_MLSTM_HEADS), jnp.float32))


def even_proj(xm, w_in, conv_qk, b_igate, b_fgate):
    bsz, seqlen, _ = xm.shape
    proj = xm @ w_in
    q, k, v, o, z_m, gates, u_p, z_p = jnp.split(proj, EVEN_SPLITS, axis=-1)
    qk = jax.nn.silu(dwconv_centered(jnp.concatenate([q, k], -1), conv_qk))
    q, k = jnp.split(qk, 2, axis=-1)

    def heads(a):
        return a.reshape(bsz, seqlen, N_MLSTM_HEADS, MLSTM_HEAD_DIM).transpose(0, 2, 1, 3)

    q, k, v = heads(q), heads(k) * (MLSTM_HEAD_DIM ** -0.5), heads(v)
    gates = gates.astype(jnp.float32).reshape(bsz, seqlen, 2, N_DIR, N_MLSTM_HEADS)
    log_i = (gates[:, :, 0] + b_igate).transpose(2, 0, 3, 1)
    log_f = jax.nn.log_sigmoid(gates[:, :, 1] + b_fgate).transpose(2, 0, 3, 1)
    return q, k, v, o, z_m, log_i, log_f, u_p, z_p


def multiscale_pool(u, row_len):
    bsz, seqlen, _ = u.shape
    rows = seqlen // row_len
    ug = u.astype(jnp.float32).reshape(bsz, rows, row_len, N_POOL_GROUPS, POOL_GROUP_DIM)
    cs = jnp.pad(jnp.cumsum(ug, axis=2), ((0, 0), (0, 0), (1, 0), (0, 0), (0, 0)))
    pos = jnp.arange(row_len)
    outs = []
    for g, w in enumerate(POOL_WINDOWS):
        lo = jnp.maximum(pos - w // 2, 0)
        hi = jnp.minimum(pos + (w - 1 - w // 2), row_len - 1)
        csg = cs[:, :, :, g, :]
        s = jnp.take(csg, hi + 1, axis=2) - jnp.take(csg, lo, axis=2)
        outs.append(s / (hi - lo + 1).astype(jnp.float32)[:, None])
    pooled = jnp.stack(outs, axis=3)
    return (pooled - ug).reshape(bsz, seqlen, N_POOL_GROUPS, POOL_GROUP_DIM)


def even_out(h, o, z_m, u_p, z_p, row_len, mh_norm_g, pool_w, pool_scale):
    bsz, nh, seqlen, dh = h.shape
    h = jnp.moveaxis(h, 1, 2) * jax.nn.sigmoid(o.astype(jnp.float32)).reshape(bsz, seqlen, nh, dh)
    h = _normalize(h).reshape(bsz, seqlen, D_MLSTM) * mh_norm_g
    y_m = h * jax.nn.silu(z_m.astype(jnp.float32))
    r = multiscale_pool(u_p, row_len)
    y_p = jnp.einsum('blgc,gcd->blgd', r, pool_w.astype(jnp.float32)).reshape(bsz, seqlen, D_POOL)
    y_p = y_p * pool_scale * jax.nn.silu(z_p.astype(jnp.float32))
    return jnp.concatenate([y_m, y_p], axis=-1).astype(u_p.dtype)


def sgu_mixer(xm, w_in, ln_g, ln_b, w_sp, b_sp):
    bsz, seqlen, _ = xm.shape
    u, v, z = jnp.split(xm @ w_in, 3, axis=-1)
    v = layer_norm(v, ln_g, ln_b)
    nc = seqlen // SGU_CHUNK
    vh = v.reshape(bsz, nc, SGU_CHUNK, N_SGU_HEADS, SGU_HEAD_DIM)
    s = jnp.einsum('hts,bnshc->bnthc', w_sp, vh) + b_sp.T[None, None, :, :, None]
    return u * s.reshape(bsz, seqlen, D_SGU) * jax.nn.silu(z)


def setup_inputs(seed: int = 0) -> dict:
    key = jax.random.key(seed)
    ks = jax.random.split(key, 24)

    def nrm(k, shape, s):
        return jax.random.normal(k, shape, jnp.float32) * s

    d = D_MODEL
    fgate_base = jnp.linspace(3.0, 6.0, N_MLSTM_HEADS, dtype=jnp.float32)
    return {
        "x": nrm(ks[0], (BATCH, SEQ, d), 1.0),
        "c": nrm(ks[1], (BATCH, d), 1.0),
        "ctx": nrm(ks[2], (BATCH, CTX_LEN, d), 1.0),
        "c_ctx": nrm(ks[3], (d,), 1.0),
        "ada_w": nrm(ks[4], (DEPTH, d, 3 * d), 0.5 * d ** -0.5),
        "ada_b": nrm(ks[5], (DEPTH, 3 * d), 0.02),
        "post_ln_g": 1.0 + nrm(ks[6], (DEPTH, d), 0.05),
        "post_ln_b": nrm(ks[7], (DEPTH, d), 0.02),
        "w_in_even": nrm(ks[8], (N_EVEN, d, N_IN_EVEN), d ** -0.5),
        "conv_qk": nrm(ks[9], (N_EVEN, QK_CONV, 2 * D_MLSTM), QK_CONV ** -0.5),
        "b_igate": nrm(ks[10], (N_EVEN, N_DIR, N_MLSTM_HEADS), 0.1),
        "b_fgate": fgate_base + nrm(ks[11], (N_EVEN, N_DIR, N_MLSTM_HEADS), 0.1),
        "mh_norm_g": 1.0 + nrm(ks[12], (N_EVEN, D_MLSTM), 0.05),
        "pool_w": nrm(ks[13], (N_EVEN, N_POOL_GROUPS, POOL_GROUP_DIM, POOL_GROUP_DIM), POOL_GROUP_DIM ** -0.5),
        "pool_scale": 1.0 + nrm(ks[14], (N_EVEN, D_POOL), 0.1),
        "w_out_even": nrm(ks[15], (N_EVEN, D_MLSTM + D_POOL, d), DEEPNORM_BETA * (D_MLSTM + D_POOL) ** -0.5),
        "w_in_odd": nrm(ks[16], (N_ODD, d, N_IN_ODD), d ** -0.5),
        "sgu_ln_g": 1.0 + nrm(ks[17], (N_ODD, D_SGU), 0.05),
        "sgu_ln_b": nrm(ks[18], (N_ODD, D_SGU), 0.02),
        "w_sp": nrm(ks[19], (N_ODD, N_SGU_HEADS, SGU_CHUNK, SGU_CHUNK), SGU_CHUNK ** -0.5),
        "b_sp": 1.0 + nrm(ks[20], (N_ODD, N_SGU_HEADS, SGU_CHUNK), 0.1),
        "w_out_odd": nrm(ks[21], (N_ODD, D_SGU, d), DEEPNORM_BETA * D_SGU ** -0.5),
    }


def reference(x, c, ctx, c_ctx, ada_w, ada_b, post_ln_g, post_ln_b, w_in_even, conv_qk, b_igate,
              b_fgate, mh_norm_g, pool_w, pool_scale, w_out_even, w_in_odd, sgu_ln_g, sgu_ln_b,
              w_sp, b_sp, w_out_odd):
    x_lat, x_ctx = x, ctx
    bsz = x.shape[0]
    for layer in range(DEPTH):
        last = layer == DEPTH - 1
        j = layer // 2
        sh_l, sc_l, g_l = ada_params(c, ada_w[layer], ada_b[layer])
        sh_l, sc_l, g_l = sh_l[:, None], sc_l[:, None], g_l[:, None]
        sh_c, sc_c, g_c = ada_params(c_ctx, ada_w[layer], ada_b[layer])
        xm_l = x_lat * (1.0 + sc_l) + sh_l
        xm_c = x_ctx * (1.0 + sc_c) + sh_c
        if layer % 2 == 0:
            ql, kl, vl, ol, zml, lil, lfl, upl, zpl = even_proj(xm_l, w_in_even[j], conv_qk[j], b_igate[j], b_fgate[j])
            qc, kc, vc, oc, zmc, lic, lfc, upc, zpc = even_proj(xm_c, w_in_even[j], conv_qk[j], b_igate[j], b_fgate[j])
            flip = lambda a: jnp.flip(a, axis=2)
            zero = zero_mlstm_state(bsz)
            hc_f, st_f = mlstm_scan(qc, kc, vc, lic[0], lfc[0], zero)
            hl_f, _ = mlstm_scan(ql, kl, vl, lil[0], lfl[0], st_f)
            hc_b, st_b = mlstm_scan(flip(qc), flip(kc), flip(vc), flip(lic[1]), flip(lfc[1]), zero)
            hl_b, _ = mlstm_scan(flip(ql), flip(kl), flip(vl), flip(lil[1]), flip(lfl[1]), st_b)
            y_l = even_out(hl_f + flip(hl_b), ol, zml, upl, zpl, GRID_W, mh_norm_g[j], pool_w[j], pool_scale[j])
            x_lat_new = layer_norm(DEEPNORM_ALPHA * x_lat + g_l * (y_l @ w_out_even[j]), post_ln_g[layer], post_ln_b[layer])
            if not last:
                y_c = even_out(hc_f + flip(hc_b), oc, zmc, upc, zpc, x_ctx.shape[1], mh_norm_g[j], pool_w[j], pool_scale[j])
                x_ctx = layer_norm(DEEPNORM_ALPHA * x_ctx + g_c * (y_c @ w_out_even[j]), post_ln_g[layer], post_ln_b[layer])
            x_lat = x_lat_new
        else:
            y_l = sgu_mixer(xm_l, w_in_odd[j], sgu_ln_g[j], sgu_ln_b[j], w_sp[j], b_sp[j])
            x_lat = layer_norm(DEEPNORM_ALPHA * x_lat + g_l * (y_l @ w_out_odd[j]), post_ln_g[layer], post_ln_b[layer])
            if not last:
                y_c = sgu_mixer(xm_c, w_in_odd[j], sgu_ln_g[j], sgu_ln_b[j], w_sp[j], b_sp[j])
                x_ctx = layer_norm(DEEPNORM_ALPHA * x_ctx + g_c * (y_c @ w_out_odd[j]), post_ln_g[layer], post_ln_b[layer])
    return x_lat
```

```python
import functools

import jax
import jax.numpy as jnp
import numpy as np
from jax import lax
from jax.experimental import pallas as pl
from jax.experimental.pallas import tpu as pltpu

F32 = jnp.float32
BF16 = jnp.bfloat16

D_MODEL = 2048
SEQ = 2048
CTX_LEN = 256
DEPTH = 2
GRID_W = 64

D_MLSTM = 1024
N_HEADS = 4
HEAD_DIM = 256
N_DIR = 2
D_POOL = 1024
POOL_WINDOWS = (2, 4, 8, 16)
POOL_GROUP_DIM = 256
N_GATE = 16
N_MAIN_EVEN = 5 * D_MLSTM + 2 * D_POOL
D_SGU = 2048
N_SGU_HEADS = 8
SGU_HEAD_DIM = 256
SGU_CHUNK = 128
DEEPNORM_ALPHA = (2.0 * DEPTH) ** 0.25
LN_EPS = 1e-5

LANES = 128
MLSTM_CHUNK = 256
STATE_W = HEAD_DIM + LANES
VMEM_LIMIT = 56 * 1024 * 1024


def _params(*sem):
    return pltpu.CompilerParams(dimension_semantics=sem, vmem_limit_bytes=VMEM_LIMIT)


def _silu(x):
    return x * jax.nn.sigmoid(x)


def _log_sigmoid(x):
    return jnp.minimum(x, 0.0) - jnp.log(1.0 + jnp.exp(-jnp.abs(x)))


def _ada_kernel(cond_ref, w_ref, b_ref, o_ref):
    a = _silu(cond_ref[...])
    o_ref[0] = jnp.dot(a.astype(BF16), w_ref[0].astype(BF16), preferred_element_type=F32) + b_ref[0]


def _ada(cond8, ada_w, ada_b):
    tn = 768
    n3 = 3 * D_MODEL
    return pl.pallas_call(
        _ada_kernel,
        grid=(DEPTH, n3 // tn),
        in_specs=[pl.BlockSpec((8, D_MODEL), lambda l, j: (0, 0)),
                  pl.BlockSpec((1, D_MODEL, tn), lambda l, j: (l, 0, j)),
                  pl.BlockSpec((1, 1, tn), lambda l, j: (l, 0, j))],
        out_specs=pl.BlockSpec((1, 8, tn), lambda l, j: (l, 0, j)),
        out_shape=jax.ShapeDtypeStruct((DEPTH, 8, n3), F32),
        compiler_params=_params("arbitrary", "arbitrary"),
        name="ada",
    )(cond8, ada_w, ada_b.reshape(DEPTH, 1, n3))


def _inproj_kernel(x_ref, sc_ref, sh_ref, w_ref, *rest, has_gates):
    if has_gates:
        wg_ref, wgt_ref, o_ref, g_ref, gt_ref, xm_ref = rest
    else:
        o_ref, xm_ref = rest

    @pl.when(pl.program_id(1) == 0)
    def _():
        xm = (x_ref[...] * (1.0 + sc_ref[0]) + sh_ref[0]).astype(BF16)
        xm_ref[...] = xm
        if has_gates:
            g_ref[...] = jnp.dot(xm, wg_ref[...], preferred_element_type=F32)
            gt_ref[...] = lax.dot_general(wgt_ref[...], xm, (((1,), (1,)), ((), ())),
                                          preferred_element_type=F32)

    o_ref[...] = jnp.dot(xm_ref[...], w_ref[...], preferred_element_type=F32).astype(o_ref.dtype)


def _inproj(x2, scale, shift, w, tm, tn, rows_per_mod, wg=None, wgt=None):
    m, n = x2.shape[0], w.shape[1]
    has_gates = wg is not None
    mod_idx = lambda i, j: ((i * tm) // rows_per_mod, 0, 0)
    in_specs = [pl.BlockSpec((tm, D_MODEL), lambda i, j: (i, 0)),
                pl.BlockSpec((1, 1, D_MODEL), mod_idx),
                pl.BlockSpec((1, 1, D_MODEL), mod_idx),
                pl.BlockSpec((D_MODEL, tn), lambda i, j: (0, j))]
    out_specs = [pl.BlockSpec((tm, tn), lambda i, j: (i, j))]
    out_shape = [jax.ShapeDtypeStruct((m, n), BF16)]
    args = [x2, scale, shift, w]
    if has_gates:
        in_specs += [pl.BlockSpec((D_MODEL, LANES), lambda i, j: (0, 0)),
                     pl.BlockSpec((N_GATE, D_MODEL), lambda i, j: (0, 0))]
        out_specs += [pl.BlockSpec((tm, LANES), lambda i, j: (i, 0)),
                      pl.BlockSpec((N_GATE, tm), lambda i, j: (0, i))]
        out_shape += [jax.ShapeDtypeStruct((m, LANES), F32), jax.ShapeDtypeStruct((N_GATE, m), F32)]
        args += [wg, wgt]
    return pl.pallas_call(
        functools.partial(_inproj_kernel, has_gates=has_gates),
        grid=(m // tm, n // tn),
        in_specs=in_specs, out_specs=out_specs, out_shape=out_shape,
        scratch_shapes=[pltpu.VMEM((tm, D_MODEL), BF16)],
        compiler_params=_params("arbitrary", "arbitrary"),
        name="inproj_gates" if has_gates else "inproj",
    )(*args)


def _split_bf16(x):
    hi = x.astype(BF16)
    return hi, (x - hi.astype(F32)).astype(BF16)


def _gates_kernel(g_ref, gt_ref, bcol_ref, brow_ref, oc_ref, or_ref, *, n, ch):
    r = lax.broadcasted_iota(jnp.int32, (ch, ch), 0)
    c = lax.broadcasted_iota(jnp.int32, (ch, ch), 1)
    tri_lo = jnp.where(r >= c, 1.0, 0.0).astype(BF16)
    tri_up = jnp.where(r <= c, 1.0, 0.0).astype(BF16)
    lane = lax.broadcasted_iota(jnp.int32, (ch, LANES), 1)
    sub = lax.broadcasted_iota(jnp.int32, (N_GATE, ch), 0)
    for k in range(n // ch):
        g = g_ref[0, k * ch:(k + 1) * ch, :] + bcol_ref[...]
        hi, lo = _split_bf16(_log_sigmoid(g))
        cf = jnp.dot(tri_lo, hi, preferred_element_type=F32) + jnp.dot(tri_lo, lo, preferred_element_type=F32)
        cb = jnp.dot(tri_up, hi, preferred_element_type=F32) + jnp.dot(tri_up, lo, preferred_element_type=F32)
        oc_ref[0, k * ch:(k + 1) * ch, :] = jnp.where(lane < 8, g, jnp.where(lane < 12, cf, cb))

        gt = gt_ref[:, k * ch:(k + 1) * ch] + brow_ref[...]
        hi, lo = _split_bf16(_log_sigmoid(gt))
        cf = jnp.dot(hi, tri_up, preferred_element_type=F32) + jnp.dot(lo, tri_up, preferred_element_type=F32)
        cb = jnp.dot(hi, tri_lo, preferred_element_type=F32) + jnp.dot(lo, tri_lo, preferred_element_type=F32)
        or_ref[0, :, k * ch:(k + 1) * ch] = jnp.where(sub < 8, gt, jnp.where(sub < 12, cf, cb))


def _gates(g, gt, bcol, brow, bsz, n):
    ch = MLSTM_CHUNK
    return pl.pallas_call(
        functools.partial(_gates_kernel, n=n, ch=ch),
        grid=(bsz,),
        in_specs=[pl.BlockSpec((1, n, LANES), lambda b: (b, 0, 0)),
                  pl.BlockSpec((N_GATE, n), lambda b: (0, b)),
                  pl.BlockSpec((1, LANES), lambda b: (0, 0)),
                  pl.BlockSpec((N_GATE, 1), lambda b: (0, 0))],
        out_specs=[pl.BlockSpec((1, n, LANES), lambda b: (b, 0, 0)),
                   pl.BlockSpec((1, N_GATE, n), lambda b: (b, 0, 0))],
        out_shape=[jax.ShapeDtypeStruct((bsz, n, LANES), F32), jax.ShapeDtypeStruct((bsz, N_GATE, n), F32)],
        compiler_params=_params("arbitrary"),
        name="gates",
    )(g.reshape(bsz, n, LANES), gt, bcol, brow)


def _conv_silu(src_ref, w_ref, dst_ref, n, out_scale):
    tb = 256
    w = w_ref[...]
    row = lax.broadcasted_iota(jnp.int32, (tb, HEAD_DIM), 0)
    for t in range(n // tb):
        x = src_ref[0, t * tb:(t + 1) * tb, :].astype(F32)
        if t == 0:
            prev = jnp.zeros((1, HEAD_DIM), F32)
        else:
            prev = src_ref[0, t * tb - 16:t * tb, :].astype(F32)[15:16, :]
        if t == n // tb - 1:
            nxt = jnp.zeros((1, HEAD_DIM), F32)
        else:
            nxt = src_ref[0, (t + 1) * tb:(t + 1) * tb + 16, :].astype(F32)[0:1, :]
        xp = jnp.where(row == 0, prev, pltpu.roll(x, 1, 0))
        xn = jnp.where(row == tb - 1, nxt, pltpu.roll(x, tb - 1, 0))
        y = w[0:1, :] * xp + w[1:2, :] * x + w[2:3, :] * xn
        dst_ref[t * tb:(t + 1) * tb, :] = (_silu(y) * out_scale).astype(dst_ref.dtype)


def _col(x, idx):
    lane = lax.broadcasted_iota(jnp.int32, x.shape, 1)
    return jnp.sum(jnp.where(lane == idx, x, 0.0), axis=1, keepdims=True)


def _mlstm_chunk(q, k, v, i_col, b_col, i_row, b_row, m, ct_ref, reverse, want_out):
    ch = k.shape[0]
    v_aug = jnp.concatenate([v, jnp.ones((ch, LANES), BF16)], axis=1)
    a_row = i_row - b_row
    a_col = i_col - b_col
    b_last = b_row[:, 0:1] if reverse else b_row[:, ch - 1:ch]
    ct = ct_ref[...]
    out = None
    if want_out:
        r = lax.broadcasted_iota(jnp.int32, (ch, ch), 0)
        c = lax.broadcasted_iota(jnp.int32, (ch, ch), 1)
        seen = (c >= r) if reverse else (c <= r)
        d = jnp.where(seen, b_col + a_row, -jnp.inf)
        m_t = jnp.maximum(jnp.max(d, axis=1, keepdims=True), b_col + m)
        w_intra = jnp.exp(d - m_t)
        w_inter = jnp.exp(b_col + m - m_t)
        s = lax.dot_general(q, k, (((1,), (1,)), ((), ())), preferred_element_type=F32) * w_intra
        num = (jnp.dot(s.astype(BF16), v_aug, preferred_element_type=F32)
               + w_inter * jnp.dot(q, ct.astype(BF16), preferred_element_type=F32))
        den = jnp.maximum(jnp.abs(num[:, HEAD_DIM:]), jnp.exp(-m_t))
        inv = 1.0 / den
        out = num[:, :HEAD_DIM] * jnp.concatenate([inv, inv], axis=1)
    g_col = b_last + a_col
    m_new = jnp.maximum(b_last + m, jnp.max(g_col, axis=0, keepdims=True))
    w_s = jnp.exp(g_col - m_new)
    decay = jnp.exp(b_last + m - m_new)
    vw = (v_aug.astype(F32) * w_s).astype(BF16)
    upd = lax.dot_general(k, vw, (((0,), (0,)), ((), ())), preferred_element_type=F32)
    ct_ref[...] = decay * ct + upd
    return out, m_new


def _mlstm_kernel(q_ref, k_ref, v_ref, o_ref, z_ref, kc_ref, vc_ref, gcol_ref, grow_ref, gccol_ref, gcrow_ref,
                  wq_ref, wk_ref, ng_ref, y_ref, qs_ref, ks_ref, kcs_ref, hacc_ref, st_ref):
    ch = MLSTM_CHUNK
    h = pl.program_id(1)
    nc = SEQ // ch
    ncc = CTX_LEN // ch

    _conv_silu(q_ref, wq_ref, qs_ref, SEQ, 1.0)
    _conv_silu(k_ref, wk_ref, ks_ref, SEQ, HEAD_DIM ** -0.5)
    _conv_silu(kc_ref, wk_ref, kcs_ref, CTX_LEN, HEAD_DIM ** -0.5)
    st_ref[...] = jnp.zeros(st_ref.shape, F32)
    hacc_ref[...] = jnp.zeros(hacc_ref.shape, F32)

    def gate_views(col_ref, row_ref, off, direction):
        gc = col_ref[0, pl.ds(off, ch), :]
        i_col = _col(gc, direction * N_HEADS + h)
        b_col = _col(gc, 8 + direction * N_HEADS + h)
        i_row = row_ref[0, pl.ds(direction * N_HEADS + h, 1), pl.ds(off, ch)]
        b_row = row_ref[0, pl.ds(8 + direction * N_HEADS + h, 1), pl.ds(off, ch)]
        return i_col, b_col, i_row, b_row

    m_f = jnp.zeros((1, 1), F32)
    m_b = jnp.zeros((1, 1), F32)
    for c in range(ncc):
        off_f, off_b = c * ch, (ncc - 1 - c) * ch
        _, m_f = _mlstm_chunk(None, kcs_ref[off_f:off_f + ch, :], vc_ref[0, off_f:off_f + ch, :],
                              *gate_views(gccol_ref, gcrow_ref, off_f, 0), m_f, st_ref.at[0], False, False)
        _, m_b = _mlstm_chunk(None, kcs_ref[off_b:off_b + ch, :], vc_ref[0, off_b:off_b + ch, :],
                              *gate_views(gccol_ref, gcrow_ref, off_b, 1), m_b, st_ref.at[1], True, False)

    def body(c, carry):
        m_f, m_b = carry
        off_f = pl.multiple_of(c * ch, ch)
        off_b = pl.multiple_of((nc - 1 - c) * ch, ch)
        out_f, m_f = _mlstm_chunk(qs_ref[pl.ds(off_f, ch), :], ks_ref[pl.ds(off_f, ch), :],
                                  v_ref[0, pl.ds(off_f, ch), :], *gate_views(gcol_ref, grow_ref, off_f, 0),
                                  m_f, st_ref.at[0], False, True)
        hacc_ref[pl.ds(off_f, ch), :] += out_f
        out_b, m_b = _mlstm_chunk(qs_ref[pl.ds(off_b, ch), :], ks_ref[pl.ds(off_b, ch), :],
                                  v_ref[0, pl.ds(off_b, ch), :], *gate_views(gcol_ref, grow_ref, off_b, 1),
                                  m_b, st_ref.at[1], True, True)
        hacc_ref[pl.ds(off_b, ch), :] += out_b
        return m_f, m_b

    lax.fori_loop(0, nc, body, (m_f, m_b))

    def finish(t, carry):
        off = pl.multiple_of(t * ch, ch)
        hh = hacc_ref[pl.ds(off, ch), :] * jax.nn.sigmoid(o_ref[0, pl.ds(off, ch), :].astype(F32))
        mu = jnp.mean(hh, axis=1, keepdims=True)
        xc = hh - mu
        var = jnp.mean(xc * xc, axis=1, keepdims=True)
        hn = xc * lax.rsqrt(var + LN_EPS) * ng_ref[...]
        y_ref[0, pl.ds(off, ch), :] = (hn * _silu(z_ref[0, pl.ds(off, ch), :].astype(F32))).astype(y_ref.dtype)
        return carry

    lax.fori_loop(0, nc, finish, 0)


def _mlstm(p, pc, gcol, grow, gccol, gcrow, conv_qk, norm_g, bsz):
    def colblk(n, base):
        return pl.BlockSpec((1, n, HEAD_DIM), lambda b, h: (b, 0, base + h))
    in_specs = [colblk(SEQ, 0), colblk(SEQ, 4), colblk(SEQ, 8), colblk(SEQ, 12), colblk(SEQ, 16),
                colblk(CTX_LEN, 4), colblk(CTX_LEN, 8),
                pl.BlockSpec((1, SEQ, LANES), lambda b, h: (b, 0, 0)),
                pl.BlockSpec((1, N_GATE, SEQ), lambda b, h: (b, 0, 0)),
                pl.BlockSpec((1, CTX_LEN, LANES), lambda b, h: (b, 0, 0)),
                pl.BlockSpec((1, N_GATE, CTX_LEN), lambda b, h: (b, 0, 0)),
                pl.BlockSpec((3, HEAD_DIM), lambda b, h: (0, h)),
                pl.BlockSpec((3, HEAD_DIM), lambda b, h: (0, N_HEADS + h)),
                pl.BlockSpec((1, HEAD_DIM), lambda b, h: (0, h))]
    return pl.pallas_call(
        _mlstm_kernel,
        grid=(bsz, N_HEADS),
        in_specs=in_specs,
        out_specs=pl.BlockSpec((1, SEQ, HEAD_DIM), lambda b, h: (b, 0, h)),
        out_shape=jax.ShapeDtypeStruct((bsz, SEQ, D_MLSTM), BF16),
        scratch_shapes=[pltpu.VMEM((SEQ, HEAD_DIM), BF16), pltpu.VMEM((SEQ, HEAD_DIM), BF16),
                        pltpu.VMEM((CTX_LEN, HEAD_DIM), BF16), pltpu.VMEM((SEQ, HEAD_DIM), F32),
                        pltpu.VMEM((N_DIR, HEAD_DIM, STATE_W), F32)],
        compiler_params=_params("arbitrary", "arbitrary"),
        name="mlstm",
    )(p, p, p, p, p, pc, pc, gcol, grow, gccol, gcrow, conv_qk, conv_qk, norm_g)


def _pool_consts(row_len):
    pos = np.arange(row_len)
    mats, invs = [], []
    for w in POOL_WINDOWS:
        lo = np.maximum(pos - w // 2, 0)
        hi = np.minimum(pos + (w - 1 - w // 2), row_len - 1)
        mats.append(((pos[None, :] >= lo[:, None]) & (pos[None, :] <= hi[:, None])).astype(np.float32))
        invs.append(np.broadcast_to((1.0 / (hi - lo + 1).astype(np.float32))[:, None], (row_len, POOL_GROUP_DIM)))
    return jnp.asarray(np.stack(mats), BF16), jnp.asarray(np.stack(invs), F32)


def _pool_kernel(u_ref, z_ref, pm_ref, inv_ref, w_ref, sc_ref, y_ref, *, tm, row_len):
    pm = pm_ref[0]
    inv = inv_ref[0]
    for r in range(tm // row_len):
        rows = slice(r * row_len, (r + 1) * row_len)
        u = u_ref[rows, :]
        res = jnp.dot(pm, u, preferred_element_type=F32) * inv - u.astype(F32)
        y = jnp.dot(res.astype(BF16), w_ref[0], preferred_element_type=F32)
        y_ref[rows, :] = (y * sc_ref[...] * _silu(z_ref[rows, :].astype(F32))).astype(y_ref.dtype)


def _pool(p2, pool_w, pool_scale, row_len):
    m = p2.shape[0]
    tm = 512
    pm, inv = _pool_consts(row_len)
    ub, zb = 5 * N_HEADS, 5 * N_HEADS + 4
    return pl.pallas_call(
        functools.partial(_pool_kernel, tm=tm, row_len=row_len),
        grid=(m // tm, len(POOL_WINDOWS)),
        in_specs=[pl.BlockSpec((tm, POOL_GROUP_DIM), lambda i, g: (i, ub + g)),
                  pl.BlockSpec((tm, POOL_GROUP_DIM), lambda i, g: (i, zb + g)),
                  pl.BlockSpec((1, row_len, row_len), lambda i, g: (g, 0, 0)),
                  pl.BlockSpec((1, row_len, POOL_GROUP_DIM), lambda i, g: (g, 0, 0)),
                  pl.BlockSpec((1, POOL_GROUP_DIM, POOL_GROUP_DIM), lambda i, g: (g, 0, 0)),
                  pl.BlockSpec((1, POOL_GROUP_DIM), lambda i, g: (0, g))],
        out_specs=pl.BlockSpec((tm, POOL_GROUP_DIM), lambda i, g: (i, g)),
        out_shape=jax.ShapeDtypeStruct((m, D_POOL), BF16),
        compiler_params=_params("arbitrary", "arbitrary"),
        name="pool",
    )(p2, p2, pm, inv, pool_w, pool_scale)


def _sgu_kernel(u_ref, v_ref, z_ref, g_ref, b_ref, wsp_ref, bsp_ref, y_ref, *, tm):
    v = v_ref[...].astype(F32)
    mu = jnp.mean(v, axis=1, keepdims=True)
    xc = v - mu
    var = jnp.mean(xc * xc, axis=1, keepdims=True)
    vn = (xc * lax.rsqrt(var + LN_EPS) * g_ref[...] + b_ref[...]).astype(BF16)
    for c in range(tm // SGU_CHUNK):
        rows = slice(c * SGU_CHUNK, (c + 1) * SGU_CHUNK)
        for hd in range(N_SGU_HEADS):
            cols = slice(hd * SGU_HEAD_DIM, (hd + 1) * SGU_HEAD_DIM)
            s = jnp.dot(wsp_ref[hd], vn[rows, cols], preferred_element_type=F32) + bsp_ref[:, cols]
            y_ref[rows, cols] = (u_ref[rows, cols].astype(F32) * s
                                 * _silu(z_ref[rows, cols].astype(F32))).astype(y_ref.dtype)


def _sgu(p2, ln_g, ln_b, w_sp, bsp):
    m = p2.shape[0]
    tm = 256
    blk = lambda j: pl.BlockSpec((tm, D_SGU), lambda i: (i, j))
    return pl.pallas_call(
        functools.partial(_sgu_kernel, tm=tm),
        grid=(m // tm,),
        in_specs=[blk(0), blk(1), blk(2),
                  pl.BlockSpec((1, D_SGU), lambda i: (0, 0)),
                  pl.BlockSpec((1, D_SGU), lambda i: (0, 0)),
                  pl.BlockSpec((N_SGU_HEADS, SGU_CHUNK, SGU_CHUNK), lambda i: (0, 0, 0)),
                  pl.BlockSpec((SGU_CHUNK, D_SGU), lambda i: (0, 0))],
        out_specs=pl.BlockSpec((tm, D_SGU), lambda i: (i, 0)),
        out_shape=jax.ShapeDtypeStruct((m, D_SGU), BF16),
        compiler_params=_params("arbitrary"),
        name="sgu",
    )(p2, p2, p2, ln_g, ln_b, w_sp, bsp)


def _outproj_kernel(*refs, n_lhs):
    ys = refs[:n_lhs]
    w_ref, x_ref, gate_ref, g_ref, b_ref, o_ref = refs[n_lhs:]
    acc = None
    k0 = 0
    for y_ref in ys:
        kk = y_ref.shape[1]
        part = jnp.dot(y_ref[...], w_ref[k0:k0 + kk, :], preferred_element_type=F32)
        acc = part if acc is None else acc + part
        k0 += kk
    r = DEEPNORM_ALPHA * x_ref[...] + gate_ref[0] * acc
    mu = jnp.mean(r, axis=1, keepdims=True)
    xc = r - mu
    var = jnp.mean(xc * xc, axis=1, keepdims=True)
    o_ref[...] = xc * lax.rsqrt(var + LN_EPS) * g_ref[...] + b_ref[...]


def _outproj(ys, w, x2, gate, ln_g, ln_b, rows_per_mod):
    m = x2.shape[0]
    tm = 512
    in_specs = [pl.BlockSpec((tm, y.shape[1]), lambda i: (i, 0)) for y in ys]
    in_specs += [pl.BlockSpec(w.shape, lambda i: (0, 0)),
                 pl.BlockSpec((tm, D_MODEL), lambda i: (i, 0)),
                 pl.BlockSpec((1, 1, D_MODEL), lambda i: ((i * tm) // rows_per_mod, 0, 0)),
                 pl.BlockSpec((1, D_MODEL), lambda i: (0, 0)),
                 pl.BlockSpec((1, D_MODEL), lambda i: (0, 0))]
    return pl.pallas_call(
        functools.partial(_outproj_kernel, n_lhs=len(ys)),
        grid=(m // tm,),
        in_specs=in_specs,
        out_specs=pl.BlockSpec((tm, D_MODEL), lambda i: (i, 0)),
        out_shape=jax.ShapeDtypeStruct((m, D_MODEL), F32),
        compiler_params=_params("arbitrary"),
        name="outproj",
    )(*ys, w, x2, gate, ln_g, ln_b)


def kernel(x, c, ctx, c_ctx, ada_w, ada_b, post_ln_g, post_ln_b, w_in_even, conv_qk, b_igate, b_fgate, mh_norm_g,
           pool_w, pool_scale, w_out_even, w_in_odd, sgu_ln_g, sgu_ln_b, w_sp, b_sp, w_out_odd):
    bsz = x.shape[0]
    m_lat = bsz * SEQ
    m_ctx = bsz * CTX_LEN

    cond8 = jnp.concatenate([c, c_ctx[None, :], jnp.zeros((8 - bsz - 1, D_MODEL), F32)], axis=0)
    mod = _ada(cond8, ada_w, ada_b)
    shift, scale, gate = (mod[:, :, i * D_MODEL:(i + 1) * D_MODEL] for i in range(3))

    w_in = w_in_even[0]
    w_main = jnp.concatenate([w_in[:, :5 * D_MLSTM], w_in[:, 5 * D_MLSTM + N_GATE:]], axis=1).astype(BF16)
    w_gate = w_in[:, 5 * D_MLSTM:5 * D_MLSTM + N_GATE]
    wg = jnp.pad(w_gate, ((0, 0), (0, LANES - N_GATE))).astype(BF16)
    wgt = w_gate.T.astype(BF16)
    gate_bias = jnp.concatenate([b_igate[0].reshape(-1), b_fgate[0].reshape(-1)])
    bcol = jnp.pad(gate_bias, (0, LANES - N_GATE))[None, :]
    brow = gate_bias[:, None]

    x2 = x.reshape(m_lat, D_MODEL)
    c2 = ctx.reshape(m_ctx, D_MODEL)
    lat_mod = lambda a: a[0, :bsz, None, :]
    ctx_mod = lambda a: a[0, bsz:bsz + 1, None, :]
    p, g, gt = _inproj(x2, lat_mod(scale), lat_mod(shift), w_main, 1024, 1024, SEQ, wg, wgt)
    pc, gc, gct = _inproj(c2, ctx_mod(scale), ctx_mod(shift), w_main, m_ctx, 1024, m_ctx, wg, wgt)
    gcol, grow = _gates(g, gt, bcol, brow, bsz, SEQ)
    gccol, gcrow = _gates(gc, gct, bcol, brow, bsz, CTX_LEN)
    y_m = _mlstm(p.reshape(bsz, SEQ, N_MAIN_EVEN), pc.reshape(bsz, CTX_LEN, N_MAIN_EVEN),
                 gcol, grow, gccol, gcrow, conv_qk[0], mh_norm_g[0][None, :], bsz)
    y_p = _pool(p, pool_w[0].astype(BF16), pool_scale[0][None, :], GRID_W)
    x2 = _outproj([y_m.reshape(m_lat, D_MLSTM), y_p], w_out_even[0].astype(BF16), x2, gate[0, :bsz, None, :],
                  post_ln_g[0][None, :], post_ln_b[0][None, :], SEQ)

    p1 = _inproj(x2, scale[1, :bsz, None, :], shift[1, :bsz, None, :], w_in_odd[0].astype(BF16), 1024, 1024, SEQ)[0]
    bsp = jnp.broadcast_to(b_sp[0].T[:, :, None], (SGU_CHUNK, N_SGU_HEADS, SGU_HEAD_DIM)).reshape(SGU_CHUNK, D_SGU)
    y = _sgu(p1, sgu_ln_g[0][None, :], sgu_ln_b[0][None, :], w_sp[0].astype(BF16), bsp)
    x2 = _outproj([y], w_out_odd[0].astype(BF16), x2, gate[1, :bsz, None, :],
                  post_ln_g[1][None, :], post_ln_b[1][None, :], SEQ)
    return x2.reshape(bsz, SEQ, D_MODEL)
```

```python
import functools

import jax
import jax.numpy as jnp
import numpy as np
from jax import lax
from jax.experimental import pallas as pl
from jax.experimental.pallas import tpu as pltpu

F32 = jnp.float32
BF16 = jnp.bfloat16

D_MODEL = 2048
SEQ = 2048
CTX_LEN = 256
DEPTH = 2
GRID_W = 64

D_MLSTM = 1024
N_HEADS = 4
HEAD_DIM = 256
N_DIR = 2
D_POOL = 1024
POOL_WINDOWS = (2, 4, 8, 16)
POOL_GROUP_DIM = 256
N_GATE = 16
N_MAIN_EVEN = 5 * D_MLSTM + 2 * D_POOL
D_SGU = 2048
N_SGU_HEADS = 8
SGU_HEAD_DIM = 256
SGU_CHUNK = 128
DEEPNORM_ALPHA = (2.0 * DEPTH) ** 0.25
LN_EPS = 1e-5

LANES = 128
MLSTM_CHUNK = 256
STATE_W = HEAD_DIM + LANES
VMEM_LIMIT = 56 * 1024 * 1024


def _params(*sem):
    return pltpu.CompilerParams(dimension_semantics=sem, vmem_limit_bytes=VMEM_LIMIT)


def _silu(x):
    return x * jax.nn.sigmoid(x)


def _log_sigmoid(x):
    return jnp.minimum(x, 0.0) - jnp.log(1.0 + jnp.exp(-jnp.abs(x)))


def _ada_kernel(cond_ref, w_ref, b_ref, o_ref):
    a = _silu(cond_ref[...])
    o_ref[0] = jnp.dot(a.astype(BF16), w_ref[0].astype(BF16), preferred_element_type=F32) + b_ref[0]


def _ada(cond8, ada_w, ada_b):
    tn = 768
    n3 = 3 * D_MODEL
    return pl.pallas_call(
        _ada_kernel,
        grid=(DEPTH, n3 // tn),
        in_specs=[pl.BlockSpec((8, D_MODEL), lambda l, j: (0, 0)),
                  pl.BlockSpec((1, D_MODEL, tn), lambda l, j: (l, 0, j)),
                  pl.BlockSpec((1, 1, tn), lambda l, j: (l, 0, j))],
        out_specs=pl.BlockSpec((1, 8, tn), lambda l, j: (l, 0, j)),
        out_shape=jax.ShapeDtypeStruct((DEPTH, 8, n3), F32),
        compiler_params=_params("arbitrary", "arbitrary"),
        name="ada",
    )(cond8, ada_w, ada_b.reshape(DEPTH, 1, n3))


def _inproj_kernel(x_ref, sc_ref, sh_ref, w_ref, *rest, has_gates):
    if has_gates:
        wg_ref, o_ref, g_ref, xm_ref = rest
    else:
        o_ref, xm_ref = rest

    @pl.when(pl.program_id(1) == 0)
    def _():
        xm = (x_ref[...] * (1.0 + sc_ref[0]) + sh_ref[0]).astype(BF16)
        xm_ref[...] = xm
        if has_gates:
            g_ref[...] = jnp.dot(xm, wg_ref[...], preferred_element_type=F32)

    o_ref[...] = jnp.dot(xm_ref[...], w_ref[...], preferred_element_type=F32).astype(o_ref.dtype)


def _inproj(x2, scale, shift, w, tm, tn, rows_per_mod, wg=None):
    m, n = x2.shape[0], w.shape[1]
    has_gates = wg is not None
    mod_idx = lambda i, j: ((i * tm) // rows_per_mod, 0, 0)
    in_specs = [pl.BlockSpec((tm, D_MODEL), lambda i, j: (i, 0)),
                pl.BlockSpec((1, 1, D_MODEL), mod_idx),
                pl.BlockSpec((1, 1, D_MODEL), mod_idx),
                pl.BlockSpec((D_MODEL, tn), lambda i, j: (0, j))]
    out_specs = [pl.BlockSpec((tm, tn), lambda i, j: (i, j))]
    out_shape = [jax.ShapeDtypeStruct((m, n), BF16)]
    args = [x2, scale, shift, w]
    if has_gates:
        in_specs += [pl.BlockSpec((D_MODEL, LANES), lambda i, j: (0, 0))]
        out_specs += [pl.BlockSpec((tm, LANES), lambda i, j: (i, 0))]
        out_shape += [jax.ShapeDtypeStruct((m, LANES), F32)]
        args += [wg]
    return pl.pallas_call(
        functools.partial(_inproj_kernel, has_gates=has_gates),
        grid=(m // tm, n // tn),
        in_specs=in_specs, out_specs=out_specs, out_shape=out_shape,
        scratch_shapes=[pltpu.VMEM((tm, D_MODEL), BF16)],
        compiler_params=_params("arbitrary", "arbitrary"),
        name="inproj_gates" if has_gates else "inproj",
    )(*args)


def _split_bf16(x):
    hi = x.astype(BF16)
    return hi, (x - hi.astype(F32)).astype(BF16)


def _scan_max(x, reverse):
    n = x.shape[0]
    row = lax.broadcasted_iota(jnp.int32, x.shape, 0)
    k = 1
    while k < n:
        if reverse:
            shifted = jnp.where(row < n - k, pltpu.roll(x, n - k, 0), -jnp.inf)
        else:
            shifted = jnp.where(row >= k, pltpu.roll(x, k, 0), -jnp.inf)
        x = jnp.maximum(x, shifted)
        k *= 2
    return x


def _gates_kernel(g_ref, bias_ref, oc_ref, or_ref, *, n, ch):
    r = lax.broadcasted_iota(jnp.int32, (ch, ch), 0)
    c = lax.broadcasted_iota(jnp.int32, (ch, ch), 1)
    tri_lo = jnp.where(r >= c, 1.0, 0.0).astype(BF16)
    tri_up = jnp.where(r <= c, 1.0, 0.0).astype(BF16)
    lane = lax.broadcasted_iota(jnp.int32, (ch, LANES), 1)
    for k in range(n // ch):
        rows = slice(k * ch, (k + 1) * ch)
        g = g_ref[0, rows, :] + bias_ref[...]
        hi, lo = _split_bf16(_log_sigmoid(g))
        cf = jnp.dot(tri_lo, hi, preferred_element_type=F32) + jnp.dot(tri_lo, lo, preferred_element_type=F32)
        cb = jnp.dot(tri_up, hi, preferred_element_type=F32) + jnp.dot(tri_up, lo, preferred_element_type=F32)
        b = jnp.where(lane < 12, cf, cb)
        a = g - pltpu.roll(b, LANES - 8, 1)
        amax = jnp.where(lane < N_HEADS, _scan_max(a, False), _scan_max(a, True))
        oc_ref[0, rows, :] = jnp.where(lane < 8, amax, b)
        or_ref[0, :, rows] = jnp.where(lane < 8, a, b).T[:N_GATE, :]


def _gates(g, bias, bsz, n):
    ch = MLSTM_CHUNK
    return pl.pallas_call(
        functools.partial(_gates_kernel, n=n, ch=ch),
        grid=(bsz,),
        in_specs=[pl.BlockSpec((1, n, LANES), lambda b: (b, 0, 0)),
                  pl.BlockSpec((1, LANES), lambda b: (0, 0))],
        out_specs=[pl.BlockSpec((1, n, LANES), lambda b: (b, 0, 0)),
                   pl.BlockSpec((1, N_GATE, n), lambda b: (b, 0, 0))],
        out_shape=[jax.ShapeDtypeStruct((bsz, n, LANES), F32), jax.ShapeDtypeStruct((bsz, N_GATE, n), F32)],
        compiler_params=_params("arbitrary"),
        name="gates",
    )(g.reshape(bsz, n, LANES), bias)


def _conv_silu(src_ref, w_ref, dst_ref, n, out_scale):
    tb = 256
    w = w_ref[...]
    row = lax.broadcasted_iota(jnp.int32, (tb, HEAD_DIM), 0)
    for t in range(n // tb):
        x = src_ref[0, t * tb:(t + 1) * tb, :].astype(F32)
        if t == 0:
            prev = jnp.zeros((1, HEAD_DIM), F32)
        else:
            prev = src_ref[0, t * tb - 16:t * tb, :].astype(F32)[15:16, :]
        if t == n // tb - 1:
            nxt = jnp.zeros((1, HEAD_DIM), F32)
        else:
            nxt = src_ref[0, (t + 1) * tb:(t + 1) * tb + 16, :].astype(F32)[0:1, :]
        xp = jnp.where(row == 0, prev, pltpu.roll(x, 1, 0))
        xn = jnp.where(row == tb - 1, nxt, pltpu.roll(x, tb - 1, 0))
        y = w[0:1, :] * xp + w[1:2, :] * x + w[2:3, :] * xn
        dst_ref[t * tb:(t + 1) * tb, :] = (_silu(y) * out_scale).astype(dst_ref.dtype)


def _lane_select_matrix(col_a, col_b):
    j = lax.broadcasted_iota(jnp.int32, (2 * LANES, 2 * LANES), 0) & (LANES - 1)
    lane = lax.broadcasted_iota(jnp.int32, (2 * LANES, 2 * LANES), 1)
    return jnp.where(j == jnp.where(lane < LANES, col_a, col_b), 1.0, 0.0).astype(BF16)


def _tile_lanes(x, reps):
    return x if reps == 1 else jnp.concatenate([x] * reps, axis=1)


def _mlstm_chunk(q, k, v, gc, a_row, b_row, m, ct_ref, sel, reverse, want_out):
    ch = k.shape[0]
    v_aug = jnp.concatenate([v, jnp.ones((ch, LANES), BF16)], axis=1)
    b_last = b_row[:, 0:1] if reverse else b_row[:, ch - 1:ch]
    ct = ct_ref[...]
    out = None
    if want_out:
        hi, lo = _split_bf16(gc)
        rep = jnp.dot(jnp.concatenate([hi, lo], axis=1), sel, preferred_element_type=F32)
        c = jnp.maximum(rep[:, :LANES], m)
        exp_neg_mt = jnp.exp(-(rep[:, LANES:] + c))
        w_inter = jnp.exp(m - c)
        row = lax.broadcasted_iota(jnp.int32, (ch, ch), 0)
        col = lax.broadcasted_iota(jnp.int32, (ch, ch), 1)
        seen = (col >= row) if reverse else (col <= row)
        w_intra = jnp.exp(jnp.where(seen, a_row - _tile_lanes(c, ch // LANES), -jnp.inf))
        s = lax.dot_general(q, k, (((1,), (1,)), ((), ())), preferred_element_type=F32) * w_intra
        num = (jnp.dot(s.astype(BF16), v_aug, preferred_element_type=F32)
               + _tile_lanes(w_inter, STATE_W // LANES) * jnp.dot(q, ct.astype(BF16), preferred_element_type=F32))
        inv = 1.0 / jnp.maximum(jnp.abs(num[:, HEAD_DIM:]), exp_neg_mt)
        out = num[:, :HEAD_DIM] * _tile_lanes(inv, HEAD_DIM // LANES)
    g_row = b_last + a_row
    m_new = jnp.maximum(b_last + m, jnp.max(g_row, axis=1, keepdims=True))
    w_s = jnp.exp(g_row - m_new)
    decay = jnp.exp(b_last + m - m_new)
    ktw = (k.T.astype(F32) * w_s).astype(BF16)
    ct_ref[...] = decay * ct + jnp.dot(ktw, v_aug, preferred_element_type=F32)
    return out, m_new


def _mlstm_kernel(q_ref, k_ref, v_ref, o_ref, z_ref, kc_ref, vc_ref, gcol_ref, grow_ref, gccol_ref, gcrow_ref,
                  wq_ref, wk_ref, ng_ref, y_ref, qs_ref, ks_ref, kcs_ref, hacc_ref, st_ref):
    ch = MLSTM_CHUNK
    h = pl.program_id(1)
    nc = SEQ // ch
    ncc = CTX_LEN // ch

    _conv_silu(q_ref, wq_ref, qs_ref, SEQ, 1.0)
    _conv_silu(k_ref, wk_ref, ks_ref, SEQ, HEAD_DIM ** -0.5)
    _conv_silu(kc_ref, wk_ref, kcs_ref, CTX_LEN, HEAD_DIM ** -0.5)
    st_ref[...] = jnp.zeros(st_ref.shape, F32)
    hacc_ref[...] = jnp.zeros(hacc_ref.shape, F32)

    def gate_views(col_ref, row_ref, off, direction):
        gc = col_ref[0, pl.ds(off, ch), :]
        a_row = row_ref[0, pl.ds(direction * N_HEADS + h, 1), pl.ds(off, ch)]
        b_row = row_ref[0, pl.ds(8 + direction * N_HEADS + h, 1), pl.ds(off, ch)]
        return gc, a_row, b_row

    sels = [_lane_select_matrix(d * N_HEADS + h, 8 + d * N_HEADS + h) for d in range(N_DIR)]

    m_f = jnp.zeros((1, 1), F32)
    m_b = jnp.zeros((1, 1), F32)
    for c in range(ncc):
        off_f, off_b = c * ch, (ncc - 1 - c) * ch
        _, m_f = _mlstm_chunk(None, kcs_ref[off_f:off_f + ch, :], vc_ref[0, off_f:off_f + ch, :],
                              *gate_views(gccol_ref, gcrow_ref, off_f, 0), m_f, st_ref.at[0], None, False, False)
        _, m_b = _mlstm_chunk(None, kcs_ref[off_b:off_b + ch, :], vc_ref[0, off_b:off_b + ch, :],
                              *gate_views(gccol_ref, gcrow_ref, off_b, 1), m_b, st_ref.at[1], None, True, False)

    def body(c, carry):
        m_f, m_b = carry
        off_f = pl.multiple_of(c * ch, ch)
        off_b = pl.multiple_of((nc - 1 - c) * ch, ch)
        out_f, m_f = _mlstm_chunk(qs_ref[pl.ds(off_f, ch), :], ks_ref[pl.ds(off_f, ch), :],
                                  v_ref[0, pl.ds(off_f, ch), :], *gate_views(gcol_ref, grow_ref, off_f, 0),
                                  m_f, st_ref.at[0], sels[0], False, True)
        hacc_ref[pl.ds(off_f, ch), :] += out_f
        out_b, m_b = _mlstm_chunk(qs_ref[pl.ds(off_b, ch), :], ks_ref[pl.ds(off_b, ch), :],
                                  v_ref[0, pl.ds(off_b, ch), :], *gate_views(gcol_ref, grow_ref, off_b, 1),
                                  m_b, st_ref.at[1], sels[1], True, True)
        hacc_ref[pl.ds(off_b, ch), :] += out_b
        return m_f, m_b

    lax.fori_loop(0, nc, body, (m_f, m_b))

    def finish(t, carry):
        off = pl.multiple_of(t * ch, ch)
        hh = hacc_ref[pl.ds(off, ch), :] * jax.nn.sigmoid(o_ref[0, pl.ds(off, ch), :].astype(F32))
        mu = jnp.mean(hh, axis=1, keepdims=True)
        xc = hh - mu
        var = jnp.mean(xc * xc, axis=1, keepdims=True)
        hn = xc * lax.rsqrt(var + LN_EPS) * ng_ref[...]
        y_ref[0, pl.ds(off, ch), :] = (hn * _silu(z_ref[0, pl.ds(off, ch), :].astype(F32))).astype(y_ref.dtype)
        return carry

    lax.fori_loop(0, nc, finish, 0)


def _mlstm(p, pc, gcol, grow, gccol, gcrow, conv_qk, norm_g, bsz):
    def colblk(n, base):
        return pl.BlockSpec((1, n, HEAD_DIM), lambda b, h: (b, 0, base + h))
    in_specs = [colblk(SEQ, 0), colblk(SEQ, 4), colblk(SEQ, 8), colblk(SEQ, 12), colblk(SEQ, 16),
                colblk(CTX_LEN, 4), colblk(CTX_LEN, 8),
                pl.BlockSpec((1, SEQ, LANES), lambda b, h: (b, 0, 0)),
                pl.BlockSpec((1, N_GATE, SEQ), lambda b, h: (b, 0, 0)),
                pl.BlockSpec((1, CTX_LEN, LANES), lambda b, h: (b, 0, 0)),
                pl.BlockSpec((1, N_GATE, CTX_LEN), lambda b, h: (b, 0, 0)),
                pl.BlockSpec((3, HEAD_DIM), lambda b, h: (0, h)),
                pl.BlockSpec((3, HEAD_DIM), lambda b, h: (0, N_HEADS + h)),
                pl.BlockSpec((1, HEAD_DIM), lambda b, h: (0, h))]
    return pl.pallas_call(
        _mlstm_kernel,
        grid=(bsz, N_HEADS),
        in_specs=in_specs,
        out_specs=pl.BlockSpec((1, SEQ, HEAD_DIM), lambda b, h: (b, 0, h)),
        out_shape=jax.ShapeDtypeStruct((bsz, SEQ, D_MLSTM), BF16),
        scratch_shapes=[pltpu.VMEM((SEQ, HEAD_DIM), BF16), pltpu.VMEM((SEQ, HEAD_DIM), BF16),
                        pltpu.VMEM((CTX_LEN, HEAD_DIM), BF16), pltpu.VMEM((SEQ, HEAD_DIM), F32),
                        pltpu.VMEM((N_DIR, HEAD_DIM, STATE_W), F32)],
        compiler_params=_params("arbitrary", "arbitrary"),
        name="mlstm",
    )(p, p, p, p, p, pc, pc, gcol, grow, gccol, gcrow, conv_qk, conv_qk, norm_g)


def _pool_consts(row_len):
    pos = np.arange(row_len)
    mats, invs = [], []
    for w in POOL_WINDOWS:
        lo = np.maximum(pos - w // 2, 0)
        hi = np.minimum(pos + (w - 1 - w // 2), row_len - 1)
        mats.append(((pos[None, :] >= lo[:, None]) & (pos[None, :] <= hi[:, None])).astype(np.float32))
        invs.append(np.broadcast_to((1.0 / (hi - lo + 1).astype(np.float32))[:, None], (row_len, POOL_GROUP_DIM)))
    return jnp.asarray(np.stack(mats), BF16), jnp.asarray(np.stack(invs), F32)


def _pool_kernel(u_ref, z_ref, pm_ref, inv_ref, w_ref, sc_ref, y_ref, *, tm, row_len):
    pm = pm_ref[0]
    inv = inv_ref[0]
    for r in range(tm // row_len):
        rows = slice(r * row_len, (r + 1) * row_len)
        u = u_ref[rows, :]
        res = jnp.dot(pm, u, preferred_element_type=F32) * inv - u.astype(F32)
        y = jnp.dot(res.astype(BF16), w_ref[0], preferred_element_type=F32)
        y_ref[rows, :] = (y * sc_ref[...] * _silu(z_ref[rows, :].astype(F32))).astype(y_ref.dtype)


def _pool(p2, pool_w, pool_scale, row_len):
    m = p2.shape[0]
    tm = 512
    pm, inv = _pool_consts(row_len)
    ub, zb = 5 * N_HEADS, 5 * N_HEADS + 4
    return pl.pallas_call(
        functools.partial(_pool_kernel, tm=tm, row_len=row_len),
        grid=(m // tm, len(POOL_WINDOWS)),
        in_specs=[pl.BlockSpec((tm, POOL_GROUP_DIM), lambda i, g: (i, ub + g)),
                  pl.BlockSpec((tm, POOL_GROUP_DIM), lambda i, g: (i, zb + g)),
                  pl.BlockSpec((1, row_len, row_len), lambda i, g: (g, 0, 0)),
                  pl.BlockSpec((1, row_len, POOL_GROUP_DIM), lambda i, g: (g, 0, 0)),
                  pl.BlockSpec((1, POOL_GROUP_DIM, POOL_GROUP_DIM), lambda i, g: (g, 0, 0)),
                  pl.BlockSpec((1, POOL_GROUP_DIM), lambda i, g: (0, g))],
        out_specs=pl.BlockSpec((tm, POOL_GROUP_DIM), lambda i, g: (i, g)),
        out_shape=jax.ShapeDtypeStruct((m, D_POOL), BF16),
        compiler_params=_params("arbitrary", "arbitrary"),
        name="pool",
    )(p2, p2, pm, inv, pool_w, pool_scale)


def _sgu_kernel(u_ref, v_ref, z_ref, g_ref, b_ref, wsp_ref, bsp_ref, y_ref, *, tm):
    v = v_ref[...].astype(F32)
    mu = jnp.mean(v, axis=1, keepdims=True)
    xc = v - mu
    var = jnp.mean(xc * xc, axis=1, keepdims=True)
    vn = (xc * lax.rsqrt(var + LN_EPS) * g_ref[...] + b_ref[...]).astype(BF16)
    for c in range(tm // SGU_CHUNK):
        rows = slice(c * SGU_CHUNK, (c + 1) * SGU_CHUNK)
        for hd in range(N_SGU_HEADS):
            cols = slice(hd * SGU_HEAD_DIM, (hd + 1) * SGU_HEAD_DIM)
            s = jnp.dot(wsp_ref[hd], vn[rows, cols], preferred_element_type=F32) + bsp_ref[:, cols]
            y_ref[rows, cols] = (u_ref[rows, cols].astype(F32) * s
                                 * _silu(z_ref[rows, cols].astype(F32))).astype(y_ref.dtype)


def _sgu(p2, ln_g, ln_b, w_sp, bsp):
    m = p2.shape[0]
    tm = 256
    blk = lambda j: pl.BlockSpec((tm, D_SGU), lambda i: (i, j))
    return pl.pallas_call(
        functools.partial(_sgu_kernel, tm=tm),
        grid=(m // tm,),
        in_specs=[blk(0), blk(1), blk(2),
                  pl.BlockSpec((1, D_SGU), lambda i: (0, 0)),
                  pl.BlockSpec((1, D_SGU), lambda i: (0, 0)),
                  pl.BlockSpec((N_SGU_HEADS, SGU_CHUNK, SGU_CHUNK), lambda i: (0, 0, 0)),
                  pl.BlockSpec((SGU_CHUNK, D_SGU), lambda i: (0, 0))],
        out_specs=pl.BlockSpec((tm, D_SGU), lambda i: (i, 0)),
        out_shape=jax.ShapeDtypeStruct((m, D_SGU), BF16),
        compiler_params=_params("arbitrary"),
        name="sgu",
    )(p2, p2, p2, ln_g, ln_b, w_sp, bsp)


def _outproj_kernel(*refs, n_lhs):
    ys = refs[:n_lhs]
    w_ref, x_ref, gate_ref, g_ref, b_ref, o_ref = refs[n_lhs:]
    acc = None
    k0 = 0
    for y_ref in ys:
        kk = y_ref.shape[1]
        part = jnp.dot(y_ref[...], w_ref[k0:k0 + kk, :], preferred_element_type=F32)
        acc = part if acc is None else acc + part
        k0 += kk
    r = DEEPNORM_ALPHA * x_ref[...] + gate_ref[0] * acc
    mu = jnp.mean(r, axis=1, keepdims=True)
    xc = r - mu
    var = jnp.mean(xc * xc, axis=1, keepdims=True)
    o_ref[...] = xc * lax.rsqrt(var + LN_EPS) * g_ref[...] + b_ref[...]


def _outproj(ys, w, x2, gate, ln_g, ln_b, rows_per_mod):
    m = x2.shape[0]
    tm = 512
    in_specs = [pl.BlockSpec((tm, y.shape[1]), lambda i: (i, 0)) for y in ys]
    in_specs += [pl.BlockSpec(w.shape, lambda i: (0, 0)),
                 pl.BlockSpec((tm, D_MODEL), lambda i: (i, 0)),
                 pl.BlockSpec((1, 1, D_MODEL), lambda i: ((i * tm) // rows_per_mod, 0, 0)),
                 pl.BlockSpec((1, D_MODEL), lambda i: (0, 0)),
                 pl.BlockSpec((1, D_MODEL), lambda i: (0, 0))]
    return pl.pallas_call(
        functools.partial(_outproj_kernel, n_lhs=len(ys)),
        grid=(m // tm,),
        in_specs=in_specs,
        out_specs=pl.BlockSpec((tm, D_MODEL), lambda i: (i, 0)),
        out_shape=jax.ShapeDtypeStruct((m, D_MODEL), F32),
        compiler_params=_params("arbitrary"),
        name="outproj",
    )(*ys, w, x2, gate, ln_g, ln_b)


def kernel(x, c, ctx, c_ctx, ada_w, ada_b, post_ln_g, post_ln_b, w_in_even, conv_qk, b_igate, b_fgate, mh_norm_g,
           pool_w, pool_scale, w_out_even, w_in_odd, sgu_ln_g, sgu_ln_b, w_sp, b_sp, w_out_odd):
    bsz = x.shape[0]
    m_lat = bsz * SEQ
    m_ctx = bsz * CTX_LEN

    cond8 = jnp.concatenate([c, c_ctx[None, :], jnp.zeros((8 - bsz - 1, D_MODEL), F32)], axis=0)
    mod = _ada(cond8, ada_w, ada_b)
    shift, scale, gate = (mod[:, :, i * D_MODEL:(i + 1) * D_MODEL] for i in range(3))

    w_in = w_in_even[0]
    w_main = jnp.concatenate([w_in[:, :5 * D_MLSTM], w_in[:, 5 * D_MLSTM + N_GATE:]], axis=1).astype(BF16)
    w_gate = w_in[:, 5 * D_MLSTM:5 * D_MLSTM + N_GATE]
    wg = jnp.pad(w_gate, ((0, 0), (0, LANES - N_GATE))).astype(BF16)
    gate_bias = jnp.concatenate([b_igate[0].reshape(-1), b_fgate[0].reshape(-1)])
    gate_bias = jnp.pad(gate_bias, (0, LANES - N_GATE))[None, :]

    x2 = x.reshape(m_lat, D_MODEL)
    c2 = ctx.reshape(m_ctx, D_MODEL)
    lat_mod = lambda a: a[0, :bsz, None, :]
    ctx_mod = lambda a: a[0, bsz:bsz + 1, None, :]
    p, g = _inproj(x2, lat_mod(scale), lat_mod(shift), w_main, 1024, 1024, SEQ, wg)
    pc, gc = _inproj(c2, ctx_mod(scale), ctx_mod(shift), w_main, m_ctx, 1024, m_ctx, wg)
    gcol, grow = _gates(g, gate_bias, bsz, SEQ)
    gccol, gcrow = _gates(gc, gate_bias, bsz, CTX_LEN)
    y_m = _mlstm(p.reshape(bsz, SEQ, N_MAIN_EVEN), pc.reshape(bsz, CTX_LEN, N_MAIN_EVEN),
                 gcol, grow, gccol, gcrow, conv_qk[0], mh_norm_g[0][None, :], bsz)
    y_p = _pool(p, pool_w[0].astype(BF16), pool_scale[0][None, :], GRID_W)
    x2 = _outproj([y_m.reshape(m_lat, D_MLSTM), y_p], w_out_even[0].astype(BF16), x2, gate[0, :bsz, None, :],
                  post_ln_g[0][None, :], post_ln_b[0][None, :], SEQ)

    p1 = _inproj(x2, scale[1, :bsz, None, :], shift[1, :bsz, None, :], w_in_odd[0].astype(BF16), 1024, 1024, SEQ)[0]
    bsp = jnp.broadcast_to(b_sp[0].T[:, :, None], (SGU_CHUNK, N_SGU_HEADS, SGU_HEAD_DIM)).reshape(SGU_CHUNK, D_SGU)
    y = _sgu(p1, sgu_ln_g[0][None, :], sgu_ln_b[0][None, :], w_sp[0].astype(BF16), bsp)
    x2 = _outproj([y], w_out_odd[0].astype(BF16), x2, gate[1, :bsz, None, :],
                  post_ln_g[1][None, :], post_ln_b[1][None, :], SEQ)
    return x2.reshape(bsz, SEQ, D_MODEL)
```
